```python
import math
import jax, jax.numpy as jnp
from jax import lax
import numpy as np

D_MODEL = 4096
BATCH = 4
SEQ = 4096
DEPTH = 1

RWKV_WIDTH = D_MODEL // 2
RWKV_HEAD = 64
RWKV_HEADS = RWKV_WIDTH // RWKV_HEAD
DECAY_LORA = max(32, int(round(1.8 * math.sqrt(RWKV_WIDTH) / 32)) * 32)
ICLR_LORA = max(32, int(round(1.8 * math.sqrt(RWKV_WIDTH) / 32)) * 32)
GATE_LORA = max(32, int(round(0.6 * RWKV_WIDTH ** 0.8 / 32)) * 32)
DIFF_WIDTH = D_MODEL - RWKV_WIDTH
DIFF_HEAD = 64
DIFF_HEADS = DIFF_WIDTH // (2 * DIFF_HEAD)
N_BUCKETS = 32
MAX_DISTANCE = 128
Q_BLOCK = 128
N_EXPERTS = 16
EXPERT_FF = D_MODEL // 2
CAPACITY_FACTOR = 2
RMS_EPS = 1e-6
LNX_EPS = 64e-5
SUBLN_EPS = 1e-5

SHIFT_COLS = 3 * RWKV_WIDTH + DECAY_LORA + ICLR_LORA + GATE_LORA
IN_COLS = SHIFT_COLS + 3 * DIFF_WIDTH

kernel_name = "hymba_rwkv7_diffattn_ecmoe_encoder"


def rmsnorm(x, g):
    xf = x.astype(jnp.float32)
    y = xf * lax.rsqrt(jnp.mean(xf * xf, axis=-1, keepdims=True) + RMS_EPS)
    return (y * g.astype(jnp.float32)).astype(x.dtype)


def token_shift(p, mu_prev, mu_next):
    prev = jnp.pad(p[:, :-1], ((0, 0), (1, 0), (0, 0)))
    nxt = jnp.pad(p[:, 1:], ((0, 0), (0, 1), (0, 0)))
    return p + mu_prev * (prev - p) + mu_next * (nxt - p)


def wkv7_scan(r, w, k, v, kk, a, reverse):
    bsz, _, nh, n = r.shape
    xs = tuple(jnp.moveaxis(t, 1, 0) for t in (r, w, k, v, kk, a))

    def step(state, inp):
        r_t, w_t, k_t, v_t, kk_t, a_t = inp
        sa = jnp.einsum('bhvk,bhk->bhv', state, -kk_t)
        state = (state * w_t[:, :, None, :]
                 + sa[..., None] * (kk_t * a_t)[:, :, None, :]
                 + v_t[..., None] * k_t[:, :, None, :])
        y = jnp.einsum('bhvk,bhk->bhv', state, r_t)
        return state, y

    s0 = jnp.zeros((bsz, nh, n, n), jnp.float32)
    _, ys = lax.scan(step, s0, xs, reverse=reverse)
    return jnp.moveaxis(ys, 0, 1)


def head_groupnorm(y, w, b):
    bsz, s = y.shape[:2]
    mu = jnp.mean(y, axis=-1, keepdims=True)
    var = jnp.mean(jnp.square(y - mu), axis=-1, keepdims=True)
    yn = (y - mu) * lax.rsqrt(var + LNX_EPS)
    return yn.reshape(bsz, s, -1) * w + b


def rwkv7_mixer(ps, w0_f, w0_b, wd_up_f, wd_up_b, a0, a_up, g_up_f, g_up_b,
                k_k, k_a, r_k, lnx_w, lnx_b):
    ps = ps.astype(jnp.float32)
    bsz, s, _ = ps.shape
    W = RWKV_WIDTH
    f32 = lambda t: t.astype(jnp.float32)
    r = ps[..., :W]
    k = ps[..., W:2 * W]
    v = ps[..., 2 * W:3 * W]
    o = 3 * W
    wd = ps[..., o:o + DECAY_LORA]; o += DECAY_LORA
    ad = ps[..., o:o + ICLR_LORA]; o += ICLR_LORA
    gd = ps[..., o:o + GATE_LORA]

    def decay(w0, up):
        wl = -jax.nn.softplus(-(f32(w0) + jnp.tanh(wd) @ f32(up))) - 0.5
        return jnp.exp(-jnp.exp(wl))

    w_f = decay(w0_f, wd_up_f)
    w_b = decay(w0_b, wd_up_b)
    a = jax.nn.sigmoid(f32(a0) + ad @ f32(a_up))
    sg = jax.nn.sigmoid(gd)
    g_f = sg @ f32(g_up_f)
    g_b = sg @ f32(g_up_b)

    heads = lambda t: t.reshape(bsz, s, RWKV_HEADS, RWKV_HEAD)
    kk = heads(k * f32(k_k))
    kk = kk * lax.rsqrt(jnp.maximum(jnp.sum(kk * kk, -1, keepdims=True), 1e-24))
    k = k * (1.0 + (a - 1.0) * f32(k_a))
    rh, kh, vh, ah = heads(r), heads(k), heads(v), heads(a)

    y_f = wkv7_scan(rh, heads(w_f), kh, vh, kk, ah, reverse=False)
    y_b = wkv7_scan(rh, heads(w_b), kh, vh, kk, ah, reverse=True)
    r_k_h = f32(r_k).reshape(RWKV_HEADS, RWKV_HEAD)
    bonus = (jnp.sum(rh * kh * r_k_h, -1, keepdims=True) * vh).reshape(bsz, s, W)
    lw, lb = f32(lnx_w), f32(lnx_b)
    return (g_f * (head_groupnorm(y_f, lw, lb) + bonus)
            + g_b * (head_groupnorm(y_b, lw, lb) + bonus))


def t5_bucket(rel):
    nb = N_BUCKETS // 2
    max_exact = nb // 2
    ret = (rel > 0).astype(jnp.int32) * nb
    n = jnp.abs(rel)
    nf = jnp.maximum(n, 1).astype(jnp.float32)
    large = max_exact + (jnp.log(nf / max_exact) / math.log(MAX_DISTANCE / max_exact)
                         * (nb - max_exact)).astype(jnp.int32)
    large = jnp.minimum(large, nb - 1)
    return ret + jnp.where(n < max_exact, n, large)


def diff_attention(q, k, v, lam, subln_g, lambda_init, rel_bias):
    bsz, s, _ = q.shape
    H, dh = DIFF_HEADS, DIFF_HEAD
    q = q.astype(jnp.float32).reshape(bsz, s, H, 2, dh).transpose(0, 2, 3, 1, 4)
    k = k.astype(jnp.float32).reshape(bsz, s, H, 2, dh).transpose(0, 2, 3, 1, 4)
    v = v.astype(jnp.float32).reshape(bsz, s, H, 2 * dh).transpose(0, 2, 1, 3)
    nblk = s // Q_BLOCK
    qb = jnp.moveaxis(q.reshape(bsz, H, 2, nblk, Q_BLOCK, dh), 3, 0)
    starts = jnp.arange(nblk, dtype=jnp.int32) * Q_BLOCK
    kpos = jnp.arange(s, dtype=jnp.int32)
    scale = 1.0 / math.sqrt(dh)
    table = rel_bias.astype(jnp.float32)

    def block(args):
        q_blk, start = args
        qpos = start + jnp.arange(Q_BLOCK, dtype=jnp.int32)
        bias = table[t5_bucket(kpos[None, :] - qpos[:, None])].transpose(2, 0, 1)
        logits = jnp.einsum('bhiqd,bhikd->bhiqk', q_blk, k) * scale + bias[None, :, None]
        p = jax.nn.softmax(logits, axis=-1)
        attn = p[:, :, 0] - lam * p[:, :, 1]
        return jnp.einsum('bhqk,bhkd->bhqd', attn, v)

    out = lax.map(block, (qb, starts))
    out = jnp.moveaxis(out, 0, 2).reshape(bsz, H, s, 2 * dh).transpose(0, 2, 1, 3)
    out = out * lax.rsqrt(jnp.mean(out * out, -1, keepdims=True) + SUBLN_EPS)
    out = out * subln_g.astype(jnp.float32) * (1.0 - lambda_init)
    return out.reshape(bsz, s, DIFF_WIDTH)


def expert_choice_ffn(h, w_router, ex_gate, ex_up, ex_down):
    bsz, s, d = h.shape
    cap = CAPACITY_FACTOR * s // N_EXPERTS
    aff = jax.nn.softmax((h @ w_router).astype(jnp.float32), axis=-1)
    gate, idx = lax.top_k(jnp.swapaxes(aff, 1, 2), cap)
    xe = jax.vmap(lambda hb, ib: hb[ib])(h, idx)
    hid = (jax.nn.silu(jnp.einsum('becd,edf->becf', xe, ex_gate))
           * jnp.einsum('becd,edf->becf', xe, ex_up))
    ye = jnp.einsum('becf,efd->becd', hid, ex_down) * gate[..., None].astype(h.dtype)
    flat = (jnp.arange(bsz, dtype=jnp.int32)[:, None, None] * s + idx).reshape(-1)
    out = jax.ops.segment_sum(ye.reshape(-1, d), flat, num_segments=bsz * s)
    return out.reshape(bsz, s, d)


def setup_inputs(seed: int = 0) -> dict:
    key = jax.random.key(seed)
    ks = iter(jax.random.split(key, 40))
    L, W, D = DEPTH, RWKV_WIDTH, D_MODEL
    nrm = lambda shape, sc: jax.random.normal(next(ks), shape, jnp.float32) * sc
    uni = lambda shape, lo, hi: jax.random.uniform(next(ks), shape, jnp.float32, lo, hi)
    return {
        "x": nrm((BATCH, SEQ, D), 1.0),
        "norm1_g": 1.0 + nrm((L, D), 0.02),
        "w_in": nrm((L, D, IN_COLS), D ** -0.5),
        "mu_prev": uni((L, SHIFT_COLS), 0.05, 0.5),
        "mu_next": uni((L, SHIFT_COLS), 0.05, 0.5),
        "w0_f": uni((L, W), -6.5, -1.5),
        "w0_b": uni((L, W), -6.5, -1.5),
        "wd_up_f": nrm((L, DECAY_LORA, W), 0.5 * DECAY_LORA ** -0.5),
        "wd_up_b": nrm((L, DECAY_LORA, W), 0.5 * DECAY_LORA ** -0.5),
        "a0": nrm((L, W), 0.1),
        "a_up": nrm((L, ICLR_LORA, W), 0.05 * ICLR_LORA ** -0.5),
        "g_up_f": nrm((L, GATE_LORA, W), GATE_LORA ** -0.5),
        "g_up_b": nrm((L, GATE_LORA, W), GATE_LORA ** -0.5),
        "k_k": 0.85 + nrm((L, W), 0.02),
        "k_a": 1.0 + nrm((L, W), 0.02),
        "r_k": nrm((L, W), 0.1),
        "lnx_w": 1.0 + nrm((L, W), 0.02),
        "lnx_b": nrm((L, W), 0.01),
        "lam_q1": nrm((L, DIFF_HEAD), 0.1),
        "lam_k1": nrm((L, DIFF_HEAD), 0.1),
        "lam_q2": nrm((L, DIFF_HEAD), 0.1),
        "lam_k2": nrm((L, DIFF_HEAD), 0.1),
        "subln_g": 1.0 + nrm((L, 2 * DIFF_HEAD), 0.02),
        "w_out": nrm((L, D, D), D ** -0.5),
        "norm2_g": 1.0 + nrm((L, D), 0.02),
        "w_router": nrm((L, D, N_EXPERTS), D ** -0.5),
        "ex_gate": nrm((L, N_EXPERTS, D, EXPERT_FF), D ** -0.5),
        "ex_up": nrm((L, N_EXPERTS, D, EXPERT_FF), D ** -0.5),
        "ex_down": nrm((L, N_EXPERTS, EXPERT_FF, D), EXPERT_FF ** -0.5),
        "rel_bias": nrm((N_BUCKETS, DIFF_HEADS), 0.5),
        "final_g": 1.0 + nrm((D,), 0.02),
    }


def reference(x, norm1_g, w_in, mu_prev, mu_next, w0_f, w0_b, wd_up_f, wd_up_b,
              a0, a_up, g_up_f, g_up_b, k_k, k_a, r_k, lnx_w, lnx_b,
              lam_q1, lam_k1, lam_q2, lam_k2, subln_g, w_out, norm2_g,
              w_router, ex_gate, ex_up, ex_down, rel_bias, final_g):
    for l in range(DEPTH):
        h = rmsnorm(x, norm1_g[l])
        proj = h @ w_in[l]
        ps = token_shift(proj[..., :SHIFT_COLS], mu_prev[l], mu_next[l])
        o_rwkv = rwkv7_mixer(ps, w0_f[l], w0_b[l], wd_up_f[l], wd_up_b[l], a0[l], a_up[l],
                             g_up_f[l], g_up_b[l], k_k[l], k_a[l], r_k[l], lnx_w[l], lnx_b[l])
        qkv = proj[..., SHIFT_COLS:]
        q = qkv[..., :DIFF_WIDTH]
        k = qkv[..., DIFF_WIDTH:2 * DIFF_WIDTH]
        v = qkv[..., 2 * DIFF_WIDTH:]
        lambda_init = 0.8 - 0.6 * math.exp(-0.3 * l)
        lam = (jnp.exp(jnp.sum(lam_q1[l].astype(jnp.float32) * lam_k1[l].astype(jnp.float32)))
               - jnp.exp(jnp.sum(lam_q2[l].astype(jnp.float32) * lam_k2[l].astype(jnp.float32)))
               + lambda_init)
        o_diff = diff_attention(q, k, v, lam, subln_g[l], lambda_init, rel_bias)
        mixed = jnp.concatenate([o_rwkv, o_diff], axis=-1).astype(x.dtype)
        x = x + mixed @ w_out[l]
        x = x + expert_choice_ffn(rmsnorm(x, norm2_g[l]), w_router[l], ex_gate[l], ex_up[l], ex_down[l])
    return rmsnorm(x, final_g)
```

```python
import functools
import math

import numpy as np
import jax
import jax.numpy as jnp
from jax import lax
from jax.experimental import pallas as pl
from jax.experimental.pallas import tpu as pltpu

F32 = jnp.float32
BF16 = jnp.bfloat16

LANES = 128
HEAD = 64
PAIR = 2 * HEAD
CHUNK = 64
N_BUCKETS = 32
MAX_DISTANCE = 128
CAPACITY_FACTOR = 2
RMS_EPS = 1e-6
LNX_EPS = 64e-5
SUBLN_EPS = 1e-5
VMEM_LIMIT = 56 * 2 ** 20


def _cparams(sem, vmem=VMEM_LIMIT):
    return pltpu.CompilerParams(dimension_semantics=sem, vmem_limit_bytes=vmem)


def _tile(n, target, unit):
    best = None
    for c in range(unit, min(n, target) + 1, unit):
        if n % c == 0:
            best = c
    assert best is not None, (n, target, unit)
    return best


def _dot(a, b):
    return jnp.dot(a, b, preferred_element_type=F32)


def _dot_nt(a, b):
    return lax.dot_general(a, b, (((1,), (1,)), ((), ())), preferred_element_type=F32)


def _dot_tn(a, b):
    return lax.dot_general(a, b, (((0,), (0,)), ((), ())), preferred_element_type=F32)


def _split2(x):
    hi = x.astype(BF16)
    lo = (x - hi.astype(F32)).astype(BF16)
    return hi, lo


def _seg_sum(x, bd):
    hi, lo = _split2(x)
    outs = []
    for g in range(x.shape[1] // LANES):
        sl = slice(g * LANES, (g + 1) * LANES)
        outs.append(_dot(hi[:, sl], bd) + _dot(lo[:, sl], bd))
    return jnp.concatenate(outs, axis=1) if len(outs) > 1 else outs[0]


def _norm_matmul_kernel(x_ref, g_ref, w_ref, o_ref, h_ref):
    @pl.when(pl.program_id(1) == 0)
    def _():
        x = x_ref[...]
        ms = jnp.mean(x * x, axis=-1, keepdims=True)
        h_ref[...] = (x * lax.rsqrt(ms + RMS_EPS) * g_ref[...]).astype(h_ref.dtype)

    o_ref[...] = _dot(h_ref[...], w_ref[...]).astype(o_ref.dtype)


def _norm_matmul(x, g, w, out_dtype, tm, tn):
    t, d = x.shape
    n = w.shape[1]
    tm, tn = _tile(t, tm, 8), _tile(n, tn, LANES)
    return pl.pallas_call(
        _norm_matmul_kernel,
        grid=(t // tm, n // tn),
        in_specs=[pl.BlockSpec((tm, d), lambda i, j: (i, 0)),
                  pl.BlockSpec((1, d), lambda i, j: (0, 0)),
                  pl.BlockSpec((d, tn), lambda i, j: (0, j))],
        out_specs=pl.BlockSpec((tm, tn), lambda i, j: (i, j)),
        out_shape=jax.ShapeDtypeStruct((t, n), out_dtype),
        scratch_shapes=[pltpu.VMEM((tm, d), BF16)],
        compiler_params=_cparams(("parallel", "arbitrary")),
        name="norm_matmul",
    )(x, g, w)


def _rwkv_prep_kernel(p_ref, pp_ref, pn_ref, mup_ref, mun_ref, w0f_ref, w0b_ref, wdf_ref, wdb_ref,
                      a0_ref, aup_ref, gupf_ref, gupb_ref, kks_ref, kas_ref, rks_ref, bd_ref,
                      r_o, k_o, v_o, kk_o, a_o, lw_o, gf_o, gb_o, bonus_o, *, tiles_per_seq, width, dlp, alp):
    i = pl.program_id(0)
    ts = p_ref.shape[0]
    cr = p_ref.shape[1]
    first = (i % tiles_per_seq) == 0
    last = (i % tiles_per_seq) == tiles_per_seq - 1
    rowid = lax.broadcasted_iota(jnp.int32, (ts, 1), 0)

    def shifted(c0, c1):
        p = p_ref[:, c0:c1]
        prow = jnp.where(first, 0.0, pp_ref[7:8, c0:c1])
        nrow = jnp.where(last, 0.0, pn_ref[0:1, c0:c1])
        prev = jnp.where(rowid == 0, prow, pltpu.roll(p, 1, axis=0))
        nxt = jnp.where(rowid == ts - 1, nrow, pltpu.roll(p, ts - 1, axis=0))
        return p + mup_ref[:, c0:c1] * (prev - p) + mun_ref[:, c0:c1] * (nxt - p)

    w = width
    r = shifted(0, w)
    k = shifted(w, 2 * w)
    v = shifted(2 * w, 3 * w)
    wd = shifted(3 * w, 3 * w + dlp)
    ad = shifted(3 * w + dlp, 3 * w + dlp + alp)
    gd = shifted(3 * w + dlp + alp, cr)
    bd = bd_ref[...]

    twd = jnp.tanh(wd).astype(BF16)

    def log_decay(w0_ref, up_ref):
        u = -(w0_ref[...] + _dot(twd, up_ref[...]))
        softplus = jnp.maximum(u, 0.0) + jnp.log(1.0 + jnp.exp(-jnp.abs(u)))
        return -jnp.exp(-softplus - 0.5)

    lw_o[0] = log_decay(w0f_ref, wdf_ref)
    lw_o[1] = log_decay(w0b_ref, wdb_ref)
    a = jax.nn.sigmoid(a0_ref[...] + _dot(ad.astype(BF16), aup_ref[...]))
    sg = jax.nn.sigmoid(gd).astype(BF16)
    gf_o[...] = _dot(sg, gupf_ref[...])
    gb_o[...] = _dot(sg, gupb_ref[...])
    kk = k * kks_ref[...]
    kk = kk * lax.rsqrt(jnp.maximum(_seg_sum(kk * kk, bd), 1e-24))
    k2 = k * (1.0 + (a - 1.0) * kas_ref[...])
    r_o[...] = r
    k_o[...] = k2
    v_o[...] = v
    kk_o[...] = kk
    a_o[...] = a
    bonus_o[...] = _seg_sum(r * k2 * rks_ref[...], bd) * v


def _rwkv_prep(p, seq, width, dlp, alp, mup, mun, w0f, w0b, wdf, wdb, a0, aup, gupf, gupb, kks, kas, rks, bd, ts):
    t, cr = p.shape
    ts = min(ts, seq)
    nt = t // ts
    tps = seq // ts
    hb = ts // 8
    nhb = t // 8
    full = lambda a: pl.BlockSpec(a.shape, lambda i: (0,) * a.ndim)
    row_out = pl.BlockSpec((ts, width), lambda i: (i, 0))
    row_shape = jax.ShapeDtypeStruct((t, width), F32)
    params = (mup, mun, w0f, w0b, wdf, wdb, a0, aup, gupf, gupb, kks, kas, rks, bd)
    return pl.pallas_call(
        functools.partial(_rwkv_prep_kernel, tiles_per_seq=tps, width=width, dlp=dlp, alp=alp),
        grid=(nt,),
        in_specs=[pl.BlockSpec((ts, cr), lambda i: (i, 0)),
                  pl.BlockSpec((8, cr), lambda i: (jnp.maximum(i * hb - 1, 0), 0)),
                  pl.BlockSpec((8, cr), lambda i: (jnp.minimum((i + 1) * hb, nhb - 1), 0))]
                 + [full(a) for a in params],
        out_specs=[row_out] * 5 + [pl.BlockSpec((2, ts, width), lambda i: (0, i, 0))] + [row_out] * 3,
        out_shape=[row_shape] * 5 + [jax.ShapeDtypeStruct((2, t, width), F32)] + [row_shape] * 3,
        compiler_params=_cparams(("parallel",)),
        name="rwkv_prep",
    )(p, p, p, *params)


def _wkv_kernel(r_ref, k_ref, v_ref, kk_ref, a_ref, lw_ref, y_ref, s_ref, *, groups):
    rev = pl.program_id(0) == 1
    cg = pl.program_id(3)
    c, p = CHUNK, PAIR

    @pl.when(cg == 0)
    def _():
        s_ref[...] = jnp.zeros_like(s_ref)

    lane = lax.broadcasted_iota(jnp.int32, (1, p), 1)
    m0 = lane < HEAD
    ri = lax.broadcasted_iota(jnp.int32, (p, p), 0)
    ci = lax.broadcasted_iota(jnp.int32, (p, p), 1)
    same = (ri < c) == (ci < c)
    rl, cl = ri & (c - 1), ci & (c - 1)
    lo_i = jnp.where(rev, cl, rl)
    hi_i = jnp.where(rev, rl, cl)
    strict = same & (lo_i > hi_i)
    incl = same & (lo_i >= hi_i)
    eye = jnp.where(ri == ci, 1.0, 0.0).astype(F32)
    tr = lax.broadcasted_iota(jnp.int32, (c, c), 0)
    tc = lax.broadcasted_iota(jnp.int32, (c, c), 1)
    tri = jnp.where(jnp.where(rev, tc, tr) >= jnp.where(rev, tr, tc), 1.0, 0.0).astype(BF16)

    def stack(x):
        return jnp.concatenate([jnp.where(m0, x, 0.0), jnp.where(m0, 0.0, x)], axis=0)

    bf = lambda t: t.astype(BF16)

    def chunk(j, carry):
        jj = jnp.where(rev, groups - 1 - j, j)
        rows = pl.ds(pl.multiple_of(jj * c, c), c)
        lw = lw_ref[0, 0, rows, :]
        r = r_ref[0, rows, :]
        k = k_ref[0, rows, :]
        v = v_ref[0, rows, :]
        kk = kk_ref[0, rows, :]
        a = a_ref[0, rows, :]
        alpha = -kk
        beta = kk * a
        h1 = bf(lw)
        r1 = lw - h1.astype(F32)
        h2 = bf(r1)
        h3 = bf(r1 - h2.astype(F32))
        g = _dot(tri, h1) + _dot(tri, h2) + _dot(tri, h3)
        gtot = jnp.where(rev, g[0:1, :], g[c - 1:c, :])
        eg = jnp.exp(g)
        egx = jnp.exp(g - lw)
        ieg = jnp.exp(-g)
        etot = jnp.exp(gtot - g)
        ab = stack(alpha * egx)
        rb = stack(r * eg)
        bb = stack(beta * ieg)
        kb = stack(k * ieg)
        vs = bf(stack(v))
        kt = stack(k * etot)
        bt = stack(beta * etot)
        lhs = bf(jnp.concatenate([ab, rb], axis=0))
        amat = _dot_nt(lhs, bf(jnp.concatenate([bb, kb], axis=0)))
        a_ab = jnp.where(strict, amat[:p, :p], 0.0)
        a_ak = jnp.where(strict, amat[:p, p:], 0.0)
        a_rb = jnp.where(incl, amat[p:, :p], 0.0)
        a_rk = jnp.where(incl, amat[p:, p:], 0.0)
        tinv = eye + a_ab
        pw = a_ab
        for _ in range(int(math.log2(c)) - 1):
            pwb = bf(pw)
            pw = _dot(pwb, pwb)
            tinv = tinv + _dot(bf(tinv), bf(pw))
        s = s_ref[...]
        sw = _dot_nt(lhs, bf(s))
        av = _dot(bf(jnp.concatenate([a_ak, a_rk], axis=0)), vs)
        u = _dot(bf(tinv), bf(sw[:p] + av[:p]))
        ub = bf(u)
        y = sw[p:] + av[p:] + _dot(bf(a_rb), ub)
        y_ref[0, 0, rows, :] = y[:c] + y[c:]
        s_ref[...] = s * jnp.exp(gtot) + _dot_tn(jnp.concatenate([vs, ub], axis=0),
                                                 bf(jnp.concatenate([kt, bt], axis=0)))
        return carry

    lax.fori_loop(0, groups, chunk, 0)


def _wkv(r, k, v, kk, a, lw, groups):
    b, s, w = r.shape
    gc = min(groups, s // CHUNK)
    rows = gc * CHUNK
    ncg = s // rows
    npair = w // PAIR

    def cmap(d, cgi):
        return jnp.where(d == 1, ncg - 1 - cgi, cgi)

    tok = pl.BlockSpec((1, rows, PAIR), lambda d, bi, pi, cgi: (bi, cmap(d, cgi), pi))
    dtok = pl.BlockSpec((1, 1, rows, PAIR), lambda d, bi, pi, cgi: (d, bi, cmap(d, cgi), pi))
    return pl.pallas_call(
        functools.partial(_wkv_kernel, groups=gc),
        grid=(2, b, npair, ncg),
        in_specs=[tok] * 5 + [dtok],
        out_specs=dtok,
        out_shape=jax.ShapeDtypeStruct((2, b, s, w), F32),
        scratch_shapes=[pltpu.VMEM((PAIR, PAIR), F32)],
        compiler_params=_cparams(("parallel", "parallel", "parallel", "arbitrary")),
        name="wkv7_chunked",
    )(r, k, v, kk, a, lw)


def _rwkv_post_kernel(y_ref, gf_ref, gb_ref, bonus_ref, lw_ref, lb_ref, bd_ref, o_ref):
    bd = bd_ref[...]
    inv = 1.0 / HEAD

    def groupnorm(y):
        mu = _seg_sum(y, bd) * inv
        yc = y - mu
        var = _seg_sum(yc * yc, bd) * inv
        return yc * lax.rsqrt(var + LNX_EPS) * lw_ref[...] + lb_ref[...]

    bonus = bonus_ref[...]
    o = gf_ref[...] * (groupnorm(y_ref[0]) + bonus) + gb_ref[...] * (groupnorm(y_ref[1]) + bonus)
    o_ref[...] = o.astype(o_ref.dtype)


def _rwkv_post(y, gf, gb, bonus, lnw, lnb, bd, ts):
    _, t, w = y.shape
    ts = min(ts, t)
    row = pl.BlockSpec((ts, w), lambda i: (i, 0))
    vec = pl.BlockSpec((1, w), lambda i: (0, 0))
    return pl.pallas_call(
        _rwkv_post_kernel,
        grid=(t // ts,),
        in_specs=[pl.BlockSpec((2, ts, w), lambda i: (0, i, 0)), row, row, row, vec, vec,
                  pl.BlockSpec(bd.shape, lambda i: (0, 0))],
        out_specs=row,
        out_shape=jax.ShapeDtypeStruct((t, w), BF16),
        compiler_params=_cparams(("parallel",)),
        name="rwkv_post",
    )(y, gf, gb, bonus, lnw, lnb, bd)


def _attn_kernel(lq1_ref, lk1_ref, lq2_ref, lk2_ref, q_ref, k_ref, v_ref, band_ref, cv_ref, sg_ref,
                 o_ref, kext_ref, s_ref, *, nsub, nblk, lambda_init):
    qt = pl.program_id(2)
    seq = k_ref.shape[1]
    qb = LANES

    @pl.when(qt == 0)
    def _():
        kext_ref[:, :LANES] = k_ref[0]
        rowblk = lax.broadcasted_iota(jnp.int32, (seq, LANES), 0) // qb
        ln = lax.broadcasted_iota(jnp.int32, (seq, LANES), 1)
        onehot = jnp.where(ln < 2 * N_BUCKETS, jnp.where((ln & (N_BUCKETS - 1)) == rowblk, 1.0, 0.0), 0.0)
        kext_ref[:, LANES:] = onehot.astype(BF16)

    lam = (jnp.exp(jnp.sum(lq1_ref[...] * lk1_ref[...], axis=-1, keepdims=True))
           - jnp.exp(jnp.sum(lq2_ref[...] * lk2_ref[...], axis=-1, keepdims=True)) + lambda_init)
    lane = lax.broadcasted_iota(jnp.int32, (qb, LANES), 1)
    lb = lane & (N_BUCKETS - 1)
    cvl = cv_ref[0, 0:1, :]
    cvr = cv_ref[0, 1:2, :]

    for sub in range(nsub):
        qi = qt * nsub + sub
        qs = q_ref[0, sub * qb:(sub + 1) * qb, :] * (1.0 / math.sqrt(HEAD))
        zero = jnp.zeros_like(qs)
        q1 = jnp.where(lane < HEAD, qs, zero)
        q2 = jnp.where(lane < HEAD, zero, qs)
        ext = jnp.where(lane < 2 * N_BUCKETS,
                        jnp.where(lb < qi - 1, cvl, jnp.where(lb > qi + 1, cvr, 0.0)), 0.0).astype(BF16)
        lhs = jnp.concatenate([jnp.concatenate([q1, ext], axis=1),
                               jnp.concatenate([q2, ext], axis=1)], axis=0)
        s_ref[...] = _dot_nt(lhs, kext_ref[...])
        for dd in (-1, 0, 1):
            cb = qi + dd

            @pl.when((cb >= 0) & (cb < nblk))
            def _():
                cols = pl.ds(pl.multiple_of(cb * qb, qb), qb)
                tile = band_ref[0, dd + 1]
                s_ref[0:qb, cols] += tile
                s_ref[qb:2 * qb, cols] += tile

        sc = s_ref[...]
        mx = jnp.max(sc, axis=-1, keepdims=True)
        pe = jnp.exp(sc - mx)
        inv = 1.0 / jnp.sum(pe, axis=-1, keepdims=True)
        attn = pe[:qb] * inv[:qb] - (lam * inv[qb:]) * pe[qb:]
        o = _dot(attn.astype(BF16), v_ref[0])
        o = o * lax.rsqrt(jnp.mean(o * o, axis=-1, keepdims=True) + SUBLN_EPS)
        o = o * sg_ref[...] * (1.0 - lambda_init)
        o_ref[0, sub * qb:(sub + 1) * qb, :] = o.astype(o_ref.dtype)


def _diff_attention(qkv, lam_vecs, band, cvec, subln_g, lambda_init, nsub):
    b, s, dw3 = qkv.shape
    dw = dw3 // 3
    nh = dw // LANES
    nblk = s // LANES
    nsub = min(nsub, nblk)
    tq = nsub * LANES
    vec = pl.BlockSpec((1, HEAD), lambda bi, h, t: (0, 0))
    return pl.pallas_call(
        functools.partial(_attn_kernel, nsub=nsub, nblk=nblk, lambda_init=lambda_init),
        grid=(b, nh, s // tq),
        in_specs=[vec] * 4 + [pl.BlockSpec((1, tq, LANES), lambda bi, h, t: (bi, t, h)),
                              pl.BlockSpec((1, s, LANES), lambda bi, h, t: (bi, 0, nh + h)),
                              pl.BlockSpec((1, s, LANES), lambda bi, h, t: (bi, 0, 2 * nh + h)),
                              pl.BlockSpec((1, 3, LANES, LANES), lambda bi, h, t: (h, 0, 0, 0)),
                              pl.BlockSpec((1, 2, LANES), lambda bi, h, t: (h, 0, 0)),
                              pl.BlockSpec((1, LANES), lambda bi, h, t: (0, 0))],
        out_specs=pl.BlockSpec((1, tq, LANES), lambda bi, h, t: (bi, t, h)),
        out_shape=jax.ShapeDtypeStruct((b, s, dw), BF16),
        scratch_shapes=[pltpu.VMEM((s, 2 * LANES), BF16), pltpu.VMEM((2 * LANES, s), F32)],
        compiler_params=_cparams(("parallel", "parallel", "arbitrary")),
        name="diff_attention",
    )(*lam_vecs, qkv, qkv, qkv, band, cvec, subln_g)


def _t5_bucket_table(seq):
    rel = np.arange(-(seq - 1), seq, dtype=np.int64)
    nb = N_BUCKETS // 2
    max_exact = nb // 2
    ret = (rel > 0).astype(np.int64) * nb
    n = np.abs(rel)
    nf = np.maximum(n, 1).astype(np.float32)
    scaled = np.log(nf / np.float32(max_exact)) / np.float32(math.log(MAX_DISTANCE / max_exact)) * np.float32(nb - max_exact)
    large = np.minimum(max_exact + scaled.astype(np.int64), nb - 1)
    return ret + np.where(n < max_exact, n, large)


def _attn_bias_operands(rel_bias, seq):
    bucket = _t5_bucket_table(seq)
    centre = seq - 1
    far = LANES + 1
    assert (bucket[:centre - far + 1] == bucket[0]).all() and (bucket[centre + far:] == bucket[-1]).all()
    i = np.arange(LANES)[:, None]
    j = np.arange(LANES)[None, :]
    tiles = []
    for dd in (-1, 0, 1):
        rel = np.clip(dd * LANES + j - i, -(seq - 1), seq - 1)
        tiles.append(bucket[rel + centre])
    idx = jnp.asarray(np.stack(tiles), jnp.int32)
    table = rel_bias.astype(F32)
    band = jnp.transpose(table[idx], (3, 0, 1, 2))
    consts = jnp.stack([table[int(bucket[0])], table[int(bucket[-1])]], axis=1)
    hi = consts.astype(BF16).astype(F32)
    lo = consts - hi
    nh = table.shape[1]
    cvec = jnp.concatenate([jnp.broadcast_to(hi[:, :, None], (nh, 2, N_BUCKETS)),
                            jnp.broadcast_to(lo[:, :, None], (nh, 2, N_BUCKETS)),
                            jnp.zeros((nh, 2, LANES - 2 * N_BUCKETS), F32)], axis=-1)
    return band, cvec


def _outproj_kernel(a_ref, b_ref, wa_ref, wb_ref, x_ref, o_ref):
    o_ref[...] = x_ref[...] + _dot(a_ref[...], wa_ref[...]) + _dot(b_ref[...], wb_ref[...])


def _outproj(a, b, wa, wb, x, tm, tn):
    t, d = x.shape
    tm, tn = _tile(t, tm, 8), _tile(d, tn, LANES)
    ka, kb = a.shape[1], b.shape[1]
    return pl.pallas_call(
        _outproj_kernel,
        grid=(t // tm, d // tn),
        in_specs=[pl.BlockSpec((tm, ka), lambda i, j: (i, 0)),
                  pl.BlockSpec((tm, kb), lambda i, j: (i, 0)),
                  pl.BlockSpec((ka, tn), lambda i, j: (0, j)),
                  pl.BlockSpec((kb, tn), lambda i, j: (0, j)),
                  pl.BlockSpec((tm, tn), lambda i, j: (i, j))],
        out_specs=pl.BlockSpec((tm, tn), lambda i, j: (i, j)),
        out_shape=jax.ShapeDtypeStruct((t, d), F32),
        compiler_params=_cparams(("parallel", "parallel")),
        name="outproj_residual",
    )(a, b, wa, wb, x)


def _router_kernel(x_ref, g_ref, wh_ref, wl_ref, h_ref, affc_ref, affr_ref, *, n_exp):
    x = x_ref[...]
    ms = jnp.mean(x * x, axis=-1, keepdims=True)
    h = x * lax.rsqrt(ms + RMS_EPS) * g_ref[...]
    h_ref[...] = h
    hh, hl = _split2(h)
    wh = wh_ref[...]
    logits = _dot(hh, wh) + _dot(hl, wh) + _dot(hh, wl_ref[...])
    lane = lax.broadcasted_iota(jnp.int32, logits.shape, 1)
    logits = jnp.where(lane < n_exp, logits, -1e30)
    e = jnp.exp(logits - jnp.max(logits, axis=-1, keepdims=True))
    aff = e / jnp.sum(e, axis=-1, keepdims=True)
    affc_ref[...] = aff
    affr_ref[0] = aff.T[:affr_ref.shape[1]]


def _router(x, g, wh, wl, bsz, seq, n_exp, tm):
    t, d = x.shape
    tm = min(tm, seq)
    ep = -(-n_exp // 8) * 8
    tps = seq // tm
    return pl.pallas_call(
        functools.partial(_router_kernel, n_exp=n_exp),
        grid=(t // tm,),
        in_specs=[pl.BlockSpec((tm, d), lambda i: (i, 0)),
                  pl.BlockSpec((1, d), lambda i: (0, 0)),
                  pl.BlockSpec((d, LANES), lambda i: (0, 0)),
                  pl.BlockSpec((d, LANES), lambda i: (0, 0))],
        out_specs=[pl.BlockSpec((tm, d), lambda i: (i, 0)),
                   pl.BlockSpec((tm, LANES), lambda i: (i, 0)),
                   pl.BlockSpec((1, ep, tm), lambda i: (i // tps, 0, i % tps))],
        out_shape=[jax.ShapeDtypeStruct((t, d), F32),
                   jax.ShapeDtypeStruct((t, LANES), F32),
                   jax.ShapeDtypeStruct((bsz, ep, seq), F32)],
        compiler_params=_cparams(("parallel",)),
        name="norm_router",
    )(x, g, wh, wl)


def _topk_kernel(affr_ref, affc_ref, idx_ref, gate_ref, col_ref, *, cap, tj):
    e = pl.program_id(1)
    seq = affc_ref.shape[0]
    a_row = affr_ref[0, pl.ds(e, 1), :]
    lane = lax.broadcasted_iota(jnp.int32, (1, LANES), 1)
    col_ref[...] = jnp.sum(jnp.where(lane == e, affc_ref[...], 0.0), axis=1, keepdims=True)
    iidx = lax.broadcasted_iota(jnp.int32, (1, seq), 1)

    def count(jt, acc):
        rows = pl.ds(pl.multiple_of(jt * tj, tj), tj)
        aj = col_ref[rows, :]
        jidx = jt * tj + lax.broadcasted_iota(jnp.int32, (tj, 1), 0)
        tie = jnp.where(jidx < iidx, 1.0, 0.0)
        beats = jnp.where(aj > a_row, 1.0, jnp.where(aj == a_row, tie, 0.0))
        return acc + jnp.sum(beats, axis=0, keepdims=True)

    rank = lax.fori_loop(0, seq // tj, count, jnp.zeros((1, seq), F32))
    iidx_f = iidx.astype(F32)
    sb = min(cap, LANES)
    for s0 in range(0, cap, sb):
        slot = (s0 + lax.broadcasted_iota(jnp.int32, (sb, 1), 0)).astype(F32)
        hit = rank == slot
        idx_ref[0, 0, s0:s0 + sb, :] = jnp.sum(jnp.where(hit, iidx_f, 0.0), axis=1, keepdims=True).astype(jnp.int32)
        gate_ref[0, 0, s0:s0 + sb, :] = jnp.sum(jnp.where(hit, a_row, 0.0), axis=1, keepdims=True)


def _topk(affr, affc, n_exp, cap, tj):
    bsz, ep, seq = affr.shape
    tj = min(tj, seq)
    out = pl.BlockSpec((1, 1, cap, 1), lambda b, e: (b, e, 0, 0))
    return pl.pallas_call(
        functools.partial(_topk_kernel, cap=cap, tj=tj),
        grid=(bsz, n_exp),
        in_specs=[pl.BlockSpec((1, ep, seq), lambda b, e: (b, 0, 0)),
                  pl.BlockSpec((seq, LANES), lambda b, e: (b, 0))],
        out_specs=[out, out],
        out_shape=[jax.ShapeDtypeStruct((bsz, n_exp, cap, 1), jnp.int32),
                   jax.ShapeDtypeStruct((bsz, n_exp, cap, 1), F32)],
        scratch_shapes=[pltpu.VMEM((seq, 1), F32)],
        compiler_params=_cparams(("parallel", "arbitrary")),
        name="expert_choice_rank",
    )(affr, affc)


def _row_copy(src_hbm, dst, src_row, dst_row, sem):
    return pltpu.make_async_copy(src_hbm.at[pl.ds(src_row, 1), :], dst.at[pl.ds(dst_row, 1), :], sem)


def _gather_kernel(idx_ref, h_hbm, o_ref, buf, sem, *, cap, seq, n_exp):
    g = pl.program_id(0)
    base = (g // n_exp) * seq

    def issue(j, c):
        _row_copy(h_hbm, buf, base + idx_ref[g * cap + j], j, sem).start()
        return c

    lax.fori_loop(0, cap, issue, 0)

    def wait(j, c):
        _row_copy(h_hbm, buf, 0, j, sem).wait()
        return c

    lax.fori_loop(0, cap, wait, 0)
    o_ref[...] = buf[...].astype(o_ref.dtype)


def _gather(idx_flat, h, groups, cap, seq, n_exp):
    d = h.shape[1]
    return pl.pallas_call(
        functools.partial(_gather_kernel, cap=cap, seq=seq, n_exp=n_exp),
        grid_spec=pltpu.PrefetchScalarGridSpec(
            num_scalar_prefetch=1,
            grid=(groups,),
            in_specs=[pl.BlockSpec(memory_space=pl.ANY)],
            out_specs=pl.BlockSpec((cap, d), lambda g, idx: (g, 0)),
            scratch_shapes=[pltpu.VMEM((cap, d), F32), pltpu.SemaphoreType.DMA(())]),
        out_shape=jax.ShapeDtypeStruct((groups * cap, d), BF16),
        compiler_params=_cparams(("arbitrary",)),
        name="expert_gather",
    )(idx_flat, h)


def _ffn_kernel(x_ref, wg_ref, wu_ref, wd_ref, gate_ref, o_ref):
    f = pl.program_id(2)
    x = x_ref[...]
    hg = _dot(x, wg_ref[0])
    hu = _dot(x, wu_ref[0])
    hid = (hg * jax.nn.sigmoid(hg) * hu).astype(BF16)
    part = _dot(hid, wd_ref[0])

    @pl.when(f == 0)
    def _():
        o_ref[...] = part

    @pl.when(f > 0)
    def _():
        o_ref[...] += part

    @pl.when(f == pl.num_programs(2) - 1)
    def _():
        o_ref[...] *= gate_ref[0, 0]


def _expert_ffn(xe, wg, wu, wd, gate, bsz, cap, tf):
    n_exp, d, ff = wg.shape
    tf = min(tf, ff)
    return pl.pallas_call(
        _ffn_kernel,
        grid=(n_exp, bsz, ff // tf),
        in_specs=[pl.BlockSpec((cap, d), lambda e, b, f: (b * n_exp + e, 0)),
                  pl.BlockSpec((1, d, tf), lambda e, b, f: (e, 0, f)),
                  pl.BlockSpec((1, d, tf), lambda e, b, f: (e, 0, f)),
                  pl.BlockSpec((1, tf, d), lambda e, b, f: (e, f, 0)),
                  pl.BlockSpec((1, 1, cap, 1), lambda e, b, f: (b, e, 0, 0))],
        out_specs=pl.BlockSpec((cap, d), lambda e, b, f: (b * n_exp + e, 0)),
        out_shape=jax.ShapeDtypeStruct(xe.shape, F32),
        compiler_params=_cparams(("parallel", "parallel", "arbitrary")),
        name="expert_swiglu",
    )(xe, wg, wu, wd, gate)


def _combine_kernel(idx_ref, ye_ref, acc_in_hbm, acc_hbm, buf, sem_in, sem_out, *, cap, seq, n_exp):
    del acc_in_hbm
    g = pl.program_id(1) * n_exp + pl.program_id(0)
    base = pl.program_id(1) * seq

    def fetch(j, c):
        _row_copy(acc_hbm, buf, base + idx_ref[g * cap + j], j, sem_in).start()
        return c

    lax.fori_loop(0, cap, fetch, 0)

    def fetch_wait(j, c):
        _row_copy(acc_hbm, buf, 0, j, sem_in).wait()
        return c

    lax.fori_loop(0, cap, fetch_wait, 0)
    buf[...] += ye_ref[...]

    def put(j, c):
        tok = base + idx_ref[g * cap + j]
        pltpu.make_async_copy(buf.at[pl.ds(j, 1), :], acc_hbm.at[pl.ds(tok, 1), :], sem_out).start()
        return c

    lax.fori_loop(0, cap, put, 0)

    def put_wait(j, c):
        pltpu.make_async_copy(buf.at[pl.ds(j, 1), :], acc_hbm.at[pl.ds(0, 1), :], sem_out).wait()
        return c

    lax.fori_loop(0, cap, put_wait, 0)


def _combine(idx_flat, ye, acc, bsz, cap, seq, n_exp):
    d = acc.shape[1]
    return pl.pallas_call(
        functools.partial(_combine_kernel, cap=cap, seq=seq, n_exp=n_exp),
        grid_spec=pltpu.PrefetchScalarGridSpec(
            num_scalar_prefetch=1,
            grid=(n_exp, bsz),
            in_specs=[pl.BlockSpec((cap, d), lambda e, b, idx: (b * n_exp + e, 0)),
                      pl.BlockSpec(memory_space=pl.ANY)],
            out_specs=pl.BlockSpec(memory_space=pl.ANY),
            scratch_shapes=[pltpu.VMEM((cap, d), F32), pltpu.SemaphoreType.DMA(()), pltpu.SemaphoreType.DMA(())]),
        out_shape=jax.ShapeDtypeStruct(acc.shape, acc.dtype),
        input_output_aliases={2: 0},
        compiler_params=_cparams(("arbitrary", "arbitrary")),
        name="expert_scatter_add",
    )(idx_flat, ye, acc)


def _rmsnorm_kernel(x_ref, g_ref, o_ref):
    x = x_ref[...]
    ms = jnp.mean(x * x, axis=-1, keepdims=True)
    o_ref[...] = x * lax.rsqrt(ms + RMS_EPS) * g_ref[...]


def _rmsnorm(x, g, tm):
    t, d = x.shape
    tm = min(tm, t)
    return pl.pallas_call(
        _rmsnorm_kernel,
        grid=(t // tm,),
        in_specs=[pl.BlockSpec((tm, d), lambda i: (i, 0)), pl.BlockSpec((1, d), lambda i: (0, 0))],
        out_specs=pl.BlockSpec((tm, d), lambda i: (i, 0)),
        out_shape=jax.ShapeDtypeStruct((t, d), F32),
        compiler_params=_cparams(("parallel",)),
        name="final_rmsnorm",
    )(x, g)


def _pad_to(n, m):
    return -(-n // m) * m


def _pad_axis(a, axis, new):
    pad = [(0, 0)] * a.ndim
    pad[axis] = (0, new - a.shape[axis])
    return jnp.pad(a, pad)


def kernel(x, norm1_g, w_in, mu_prev, mu_next, w0_f, w0_b, wd_up_f, wd_up_b, a0, a_up, g_up_f, g_up_b, k_k, k_a, r_k, lnx_w, lnx_b, lam_q1, lam_k1, lam_q2, lam_k2, subln_g, w_out, norm2_g, w_router, ex_gate, ex_up, ex_down, rel_bias, final_g):
    bsz, seq, d = x.shape
    depth = norm1_g.shape[0]
    width = w0_f.shape[-1]
    dl, al, gl = wd_up_f.shape[1], a_up.shape[1], g_up_f.shape[1]
    dlp, alp, glp = (_pad_to(n, LANES) for n in (dl, al, gl))
    dw = d - width
    n_exp = w_router.shape[-1]
    cap = CAPACITY_FACTOR * seq // n_exp
    t = bsz * seq
    assert seq // LANES <= N_BUCKETS and width % PAIR == 0 and dw % LANES == 0 and seq % CHUNK == 0

    lane = np.arange(LANES)
    bd = jnp.asarray((lane[:, None] // HEAD) == (lane[None, :] // HEAD), BF16)
    row = lambda v: v.reshape(1, -1).astype(F32)
    xt = x.reshape(t, d)
    band, cvec = _attn_bias_operands(rel_bias, seq)

    for l in range(depth):
        o = 3 * width
        segs = [(o, dl, dlp), (o + dl, al, alp), (o + dl + al, gl, glp)]
        w_r = jnp.concatenate([w_in[l][:, :o]] + [_pad_axis(w_in[l][:, s:s + n], 1, npad) for s, n, npad in segs],
                              axis=1).astype(BF16)
        w_d = w_in[l][:, o + dl + al + gl:].astype(BF16)
        padmu = lambda mu: jnp.concatenate([mu[:o]] + [_pad_axis(mu[s:s + n], 0, npad) for s, n, npad in segs]
                                           ).reshape(1, -1)
        g1 = row(norm1_g[l])
        p_rwkv = _norm_matmul(xt, g1, w_r, F32, 512, 512)
        p_diff = _norm_matmul(xt, g1, w_d, BF16, 512, 1024)

        r, k2, v, kk, a, lw, gf, gb, bonus = _rwkv_prep(
            p_rwkv, seq, width, dlp, alp, padmu(mu_prev[l]), padmu(mu_next[l]), row(w0_f[l]), row(w0_b[l]),
            _pad_axis(wd_up_f[l], 0, dlp).astype(BF16), _pad_axis(wd_up_b[l], 0, dlp).astype(BF16),
            row(a0[l]), _pad_axis(a_up[l], 0, alp).astype(BF16),
            _pad_axis(g_up_f[l], 0, glp).astype(BF16), _pad_axis(g_up_b[l], 0, glp).astype(BF16),
            row(k_k[l]), row(k_a[l]), row(r_k[l]), bd, 128)
        sh = (bsz, seq, width)
        y = _wkv(r.reshape(sh), k2.reshape(sh), v.reshape(sh), kk.reshape(sh), a.reshape(sh),
                 lw.reshape(2, bsz, seq, width), 8)
        o_rwkv = _rwkv_post(y.reshape(2, t, width), gf, gb, bonus, row(lnx_w[l]), row(lnx_b[l]), bd, 256)

        lambda_init = 0.8 - 0.6 * math.exp(-0.3 * l)
        lam_vecs = [row(lam_q1[l]), row(lam_k1[l]), row(lam_q2[l]), row(lam_k2[l])]
        o_diff = _diff_attention(p_diff.reshape(bsz, seq, 3 * dw), lam_vecs, band, cvec,
                                 row(subln_g[l]), lambda_init, 2)

        wo = w_out[l].astype(BF16)
        xt = _outproj(o_rwkv, o_diff.reshape(t, dw), wo[:width], wo[width:], xt, 1024, 512)

        wr = _pad_axis(w_router[l].astype(F32), 1, LANES)
        wr_hi = wr.astype(BF16)
        wr_lo = (wr - wr_hi.astype(F32)).astype(BF16)
        h2, affc, affr = _router(xt, row(norm2_g[l]), wr_hi, wr_lo, bsz, seq, n_exp, 512)
        idx, gate = _topk(affr, affc, n_exp, cap, 128)
        idx_flat = idx.reshape(-1)
        xe = _gather(idx_flat, h2, bsz * n_exp, cap, seq, n_exp)
        ye = _expert_ffn(xe, ex_gate[l].astype(BF16), ex_up[l].astype(BF16), ex_down[l].astype(BF16),
                         gate, bsz, cap, 256)
        xt = _combine(idx_flat, ye, xt, bsz, cap, seq, n_exp)

    return _rmsnorm(xt, row(final_g), 512).reshape(bsz, seq, d)
```

```python
import functools
import math

import numpy as np
import jax
import jax.numpy as jnp
from jax import lax
from jax.experimental import pallas as pl
from jax.experimental.pallas import tpu as pltpu

F32 = jnp.float32
BF16 = jnp.bfloat16

LANES = 128
HEAD = 64
PAIR = 2 * HEAD
CHUNK = 64
N_BUCKETS = 32
MAX_DISTANCE = 128
CAPACITY_FACTOR = 2
RMS_EPS = 1e-6
LNX_EPS = 64e-5
SUBLN_EPS = 1e-5
LOG2E = math.log2(math.e)
VMEM_LIMIT = 56 * 2 ** 20
VMEM_LIMIT_MAX = 60 * 2 ** 20


def _cparams(sem, vmem=VMEM_LIMIT):
    return pltpu.CompilerParams(dimension_semantics=sem, vmem_limit_bytes=vmem)


def _tile(n, target, unit):
    best = None
    for c in range(unit, min(n, target) + 1, unit):
        if n % c == 0:
            best = c
    assert best is not None, (n, target, unit)
    return best


def _dot(a, b):
    return jnp.dot(a, b, preferred_element_type=F32)


def _dot_nt(a, b):
    return lax.dot_general(a, b, (((1,), (1,)), ((), ())), preferred_element_type=F32)


def _dot_tn(a, b):
    return lax.dot_general(a, b, (((0,), (0,)), ((), ())), preferred_element_type=F32)


def _split2(x):
    hi = x.astype(BF16)
    lo = (x - hi.astype(F32)).astype(BF16)
    return hi, lo


def _seg_sum(x, bd):
    hi, lo = _split2(x)
    outs = []
    for g in range(x.shape[1] // LANES):
        sl = slice(g * LANES, (g + 1) * LANES)
        outs.append(_dot(hi[:, sl], bd) + _dot(lo[:, sl], bd))
    return jnp.concatenate(outs, axis=1) if len(outs) > 1 else outs[0]


def _norm_matmul_kernel(x_ref, g_ref, w_ref, cs_ref, o_ref, h_ref):
    @pl.when(pl.program_id(1) == 0)
    def _():
        x = x_ref[...]
        ms = jnp.mean(x * x, axis=-1, keepdims=True)
        h_ref[...] = (x * lax.rsqrt(ms + RMS_EPS) * g_ref[...]).astype(h_ref.dtype)

    o_ref[...] = (_dot(h_ref[...], w_ref[...]) * cs_ref[...]).astype(o_ref.dtype)


def _norm_matmul(x, g, w, colscale, out_dtype, tm, tn):
    t, d = x.shape
    n = w.shape[1]
    tm, tn = _tile(t, tm, 8), _tile(n, tn, LANES)
    return pl.pallas_call(
        _norm_matmul_kernel,
        grid=(t // tm, n // tn),
        in_specs=[pl.BlockSpec((tm, d), lambda i, j: (i, 0)),
                  pl.BlockSpec((1, d), lambda i, j: (0, 0)),
                  pl.BlockSpec((d, tn), lambda i, j: (0, j)),
                  pl.BlockSpec((1, tn), lambda i, j: (0, j))],
        out_specs=pl.BlockSpec((tm, tn), lambda i, j: (i, j)),
        out_shape=jax.ShapeDtypeStruct((t, n), out_dtype),
        scratch_shapes=[pltpu.VMEM((tm, d), BF16)],
        compiler_params=_cparams(("parallel", "arbitrary")),
        name="norm_matmul",
    )(x, g, w, colscale)


def _rwkv_prep_kernel(p_ref, pp_ref, pn_ref, mup_ref, mun_ref, w0f_ref, w0b_ref, wdf_ref, wdb_ref,
                      a0_ref, aup_ref, gupf_ref, gupb_ref, kks_ref, kas_ref, rks_ref, bd_ref,
                      r_o, k_o, v_o, kk_o, a_o, lw_o, gf_o, gb_o, bonus_o, *, tiles_per_seq, width, dlp, alp):
    i = pl.program_id(0)
    ts = p_ref.shape[0]
    cr = p_ref.shape[1]
    first = (i % tiles_per_seq) == 0
    last = (i % tiles_per_seq) == tiles_per_seq - 1
    rowid = lax.broadcasted_iota(jnp.int32, (ts, 1), 0)

    def shifted(c0, c1):
        p = p_ref[:, c0:c1]
        prow = jnp.where(first, 0.0, pp_ref[7:8, c0:c1])
        nrow = jnp.where(last, 0.0, pn_ref[0:1, c0:c1])
        prev = jnp.where(rowid == 0, prow, pltpu.roll(p, 1, axis=0))
        nxt = jnp.where(rowid == ts - 1, nrow, pltpu.roll(p, ts - 1, axis=0))
        return p + mup_ref[:, c0:c1] * (prev - p) + mun_ref[:, c0:c1] * (nxt - p)

    w = width
    r = shifted(0, w)
    k = shifted(w, 2 * w)
    v = shifted(2 * w, 3 * w)
    wd = shifted(3 * w, 3 * w + dlp)
    ad = shifted(3 * w + dlp, 3 * w + dlp + alp)
    gd = shifted(3 * w + dlp + alp, cr)
    bd = bd_ref[...]

    twd = jnp.tanh(wd).astype(BF16)

    def log_decay(w0_ref, up_ref):
        u = -(w0_ref[...] + _dot(twd, up_ref[...]))
        softplus = jnp.maximum(u, 0.0) + jnp.log(1.0 + jnp.exp(-jnp.abs(u)))
        return -jnp.exp(-softplus - 0.5)

    lw_o[0] = log_decay(w0f_ref, wdf_ref)
    lw_o[1] = log_decay(w0b_ref, wdb_ref)
    a = jax.nn.sigmoid(a0_ref[...] + _dot(ad.astype(BF16), aup_ref[...]))
    sg = jax.nn.sigmoid(gd).astype(BF16)
    gf_o[...] = _dot(sg, gupf_ref[...])
    gb_o[...] = _dot(sg, gupb_ref[...])
    kk = k * kks_ref[...]
    kk = kk * lax.rsqrt(jnp.maximum(_seg_sum(kk * kk, bd), 1e-24))
    k2 = k * (1.0 + (a - 1.0) * kas_ref[...])
    r_o[...] = r
    k_o[...] = k2
    v_o[...] = v
    kk_o[...] = kk
    a_o[...] = a
    bonus_o[...] = _seg_sum(r * k2 * rks_ref[...], bd) * v


def _rwkv_prep(p, seq, width, dlp, alp, mup, mun, w0f, w0b, wdf, wdb, a0, aup, gupf, gupb, kks, kas, rks, bd, ts):
    t, cr = p.shape
    ts = min(ts, seq)
    nt = t // ts
    tps = seq // ts
    hb = ts // 8
    nhb = t // 8
    full = lambda a: pl.BlockSpec(a.shape, lambda i: (0,) * a.ndim)
    row_out = pl.BlockSpec((ts, width), lambda i: (i, 0))
    row_shape = jax.ShapeDtypeStruct((t, width), F32)
    params = (mup, mun, w0f, w0b, wdf, wdb, a0, aup, gupf, gupb, kks, kas, rks, bd)
    return pl.pallas_call(
        functools.partial(_rwkv_prep_kernel, tiles_per_seq=tps, width=width, dlp=dlp, alp=alp),
        grid=(nt,),
        in_specs=[pl.BlockSpec((ts, cr), lambda i: (i, 0)),
                  pl.BlockSpec((8, cr), lambda i: (jnp.maximum(i * hb - 1, 0), 0)),
                  pl.BlockSpec((8, cr), lambda i: (jnp.minimum((i + 1) * hb, nhb - 1), 0))]
                 + [full(a) for a in params],
        out_specs=[row_out] * 5 + [pl.BlockSpec((2, ts, width), lambda i: (0, i, 0))] + [row_out] * 3,
        out_shape=[row_shape] * 5 + [jax.ShapeDtypeStruct((2, t, width), F32)] + [row_shape] * 3,
        compiler_params=_cparams(("parallel",)),
        name="rwkv_prep",
    )(p, p, p, *params)


def _wkv_kernel(r_ref, k_ref, v_ref, kk_ref, a_ref, lw_ref, y_ref, s_ref, *, groups, npair):
    rev = pl.program_id(0) == 1
    cg = pl.program_id(3)
    c, p = CHUNK, PAIR

    @pl.when(cg == 0)
    def _():
        s_ref[...] = jnp.zeros_like(s_ref)

    lane = lax.broadcasted_iota(jnp.int32, (1, p), 1)
    m0 = lane < HEAD
    ri = lax.broadcasted_iota(jnp.int32, (p, p), 0)
    ci = lax.broadcasted_iota(jnp.int32, (p, p), 1)
    same = (ri < c) == (ci < c)
    rl, cl = ri & (c - 1), ci & (c - 1)
    lo_i = jnp.where(rev, cl, rl)
    hi_i = jnp.where(rev, rl, cl)
    strict = same & (lo_i > hi_i)
    incl = same & (lo_i >= hi_i)
    eye = jnp.where(ri == ci, 1.0, 0.0).astype(F32)
    tr = lax.broadcasted_iota(jnp.int32, (c, c), 0)
    tc = lax.broadcasted_iota(jnp.int32, (c, c), 1)
    tri = jnp.where(jnp.where(rev, tc, tr) >= jnp.where(rev, tr, tc), 1.0, 0.0).astype(BF16)

    def stack(x):
        return jnp.concatenate([jnp.where(m0, x, 0.0), jnp.where(m0, 0.0, x)], axis=0)

    bf = lambda t: t.astype(BF16)

    def chunk(j, carry):
        jj = jnp.where(rev, groups - 1 - j, j)
        rows = pl.ds(pl.multiple_of(jj * c, c), c)
        pairs = range(npair)
        each = lambda f, *cols: [f(*xs) for xs in zip(*cols)]
        ld = lambda ref, pi: ref[0, rows, pi * p:(pi + 1) * p]
        lw = [lw_ref[0, 0, rows, pi * p:(pi + 1) * p] for pi in pairs]
        r = [ld(r_ref, pi) for pi in pairs]
        k = [ld(k_ref, pi) for pi in pairs]
        v = [ld(v_ref, pi) for pi in pairs]
        kk = [ld(kk_ref, pi) for pi in pairs]
        beta = each(lambda x, y: x * y, kk, [ld(a_ref, pi) for pi in pairs])

        def cumsum(x):
            h1 = bf(x)
            r1 = x - h1.astype(F32)
            h2 = bf(r1)
            h3 = bf(r1 - h2.astype(F32))
            return _dot(tri, h1) + _dot(tri, h2) + _dot(tri, h3)

        g = each(cumsum, lw)
        gtot = each(lambda x: jnp.where(rev, x[0:1, :], x[c - 1:c, :]), g)
        eg = each(jnp.exp, g)
        egx = each(lambda x, y: jnp.exp(x - y), g, lw)
        ieg = each(lambda x: jnp.exp(-x), g)
        etot = each(lambda x, y: jnp.exp(x - y), gtot, g)
        lhs = each(lambda kk_, egx_, r_, eg_: bf(jnp.concatenate([stack(-kk_ * egx_), stack(r_ * eg_)], axis=0)),
                   kk, egx, r, eg)
        rhs = each(lambda b_, k_, ieg_: bf(jnp.concatenate([stack(b_ * ieg_), stack(k_ * ieg_)], axis=0)),
                   beta, k, ieg)
        amat = each(_dot_nt, lhs, rhs)
        a_ab = each(lambda m: jnp.where(strict, m[:p, :p], 0.0), amat)
        tinv = each(lambda m: eye + m, a_ab)
        pw = a_ab
        for _ in range(int(math.log2(c)) - 1):
            pw = each(lambda m: _dot(bf(m), bf(m)), pw)
            tinv = each(lambda t_, m: t_ + _dot(bf(t_), bf(m)), tinv, pw)
        vs = each(lambda x: bf(stack(x)), v)
        s = [s_ref[pi] for pi in pairs]
        sw = each(lambda l_, s_: _dot_nt(l_, bf(s_)), lhs, s)
        av = each(lambda m, v_: _dot(bf(jnp.concatenate([jnp.where(strict, m[:p, p:], 0.0),
                                                         jnp.where(incl, m[p:, p:], 0.0)], axis=0)), v_),
                  amat, vs)
        ub = each(lambda t_, sw_, av_: bf(_dot(bf(t_), bf(sw_[:p] + av_[:p]))), tinv, sw, av)
        y = each(lambda m, u_, sw_, av_: sw_[p:] + av_[p:] + _dot(bf(jnp.where(incl, m[p:, :p], 0.0)), u_),
                 amat, ub, sw, av)
        snew = each(lambda s_, gt, v_, u_, k_, b_, et: s_ * jnp.exp(gt) + _dot_tn(
            jnp.concatenate([v_, u_], axis=0), bf(jnp.concatenate([stack(k_ * et), stack(b_ * et)], axis=0))),
            s, gtot, vs, ub, k, beta, etot)
        for pi in pairs:
            y_ref[0, 0, rows, pi * p:(pi + 1) * p] = y[pi][:c] + y[pi][c:]
            s_ref[pi] = snew[pi]
        return carry

    lax.fori_loop(0, groups, chunk, 0)


def _wkv(r, k, v, kk, a, lw, groups, npair):
    b, s, w = r.shape
    gc = min(groups, s // CHUNK)
    rows = gc * CHUNK
    ncg = s // rows
    npair = min(npair, w // PAIR)
    lanes = npair * PAIR

    def cmap(d, cgi):
        return jnp.where(d == 1, ncg - 1 - cgi, cgi)

    tok = pl.BlockSpec((1, rows, lanes), lambda d, bi, pi, cgi: (bi, cmap(d, cgi), pi))
    dtok = pl.BlockSpec((1, 1, rows, lanes), lambda d, bi, pi, cgi: (d, bi, cmap(d, cgi), pi))
    return pl.pallas_call(
        functools.partial(_wkv_kernel, groups=gc, npair=npair),
        grid=(2, b, w // lanes, ncg),
        in_specs=[tok] * 5 + [dtok],
        out_specs=dtok,
        out_shape=jax.ShapeDtypeStruct((2, b, s, w), F32),
        scratch_shapes=[pltpu.VMEM((npair, PAIR, PAIR), F32)],
        compiler_params=_cparams(("parallel", "parallel", "parallel", "arbitrary")),
        name="wkv7_chunked",
    )(r, k, v, kk, a, lw)


def _rwkv_post_kernel(y_ref, gf_ref, gb_ref, bonus_ref, lw_ref, lb_ref, bd_ref, o_ref):
    bd = bd_ref[...]
    inv = 1.0 / HEAD

    def groupnorm(y):
        mu = _seg_sum(y, bd) * inv
        yc = y - mu
        var = _seg_sum(yc * yc, bd) * inv
        return yc * lax.rsqrt(var + LNX_EPS) * lw_ref[...] + lb_ref[...]

    bonus = bonus_ref[...]
    o = gf_ref[...] * (groupnorm(y_ref[0]) + bonus) + gb_ref[...] * (groupnorm(y_ref[1]) + bonus)
    o_ref[...] = o.astype(o_ref.dtype)


def _rwkv_post(y, gf, gb, bonus, lnw, lnb, bd, ts):
    _, t, w = y.shape
    ts = min(ts, t)
    row = pl.BlockSpec((ts, w), lambda i: (i, 0))
    vec = pl.BlockSpec((1, w), lambda i: (0, 0))
    return pl.pallas_call(
        _rwkv_post_kernel,
        grid=(t // ts,),
        in_specs=[pl.BlockSpec((2, ts, w), lambda i: (0, i, 0)), row, row, row, vec, vec,
                  pl.BlockSpec(bd.shape, lambda i: (0, 0))],
        out_specs=row,
        out_shape=jax.ShapeDtypeStruct((t, w), BF16),
        compiler_params=_cparams(("parallel",)),
        name="rwkv_post",
    )(y, gf, gb, bonus, lnw, lnb, bd)


def _attn_kernel(lq1_ref, lk1_ref, lq2_ref, lk2_ref, q_ref, k_ref, v_ref, band_ref, cv_ref, sg_ref,
                 o_ref, kext_ref, vext_ref, s_ref, *, nsub, nblk, lambda_init):
    qt = pl.program_id(2)
    seq = k_ref.shape[1]
    qb = LANES

    @pl.when(qt == 0)
    def _():
        kext_ref[:, :LANES] = k_ref[0]
        rowblk = lax.broadcasted_iota(jnp.int32, (seq, LANES), 0) // qb
        ln = lax.broadcasted_iota(jnp.int32, (seq, LANES), 1)
        onehot = jnp.where(ln < 2 * N_BUCKETS, jnp.where((ln & (N_BUCKETS - 1)) == rowblk, 1.0, 0.0), 0.0)
        kext_ref[:, LANES:] = onehot.astype(BF16)
        vext_ref[:, :LANES] = v_ref[0]
        vext_ref[:, LANES:] = jnp.ones((seq, LANES), BF16)

    lam = (jnp.exp(jnp.sum(lq1_ref[...] * lk1_ref[...], axis=-1, keepdims=True))
           - jnp.exp(jnp.sum(lq2_ref[...] * lk2_ref[...], axis=-1, keepdims=True)) + lambda_init)
    lane = lax.broadcasted_iota(jnp.int32, (qb, LANES), 1)
    lb = lane & (N_BUCKETS - 1)
    cvl = cv_ref[0, 0:1, :]
    cvr = cv_ref[0, 1:2, :]

    for sub in range(nsub):
        qi = qt * nsub + sub
        qs = q_ref[0, sub * qb:(sub + 1) * qb, :]
        zero = jnp.zeros_like(qs)
        q1 = jnp.where(lane < HEAD, qs, zero)
        q2 = jnp.where(lane < HEAD, zero, qs)
        ext = jnp.where(lane < 2 * N_BUCKETS,
                        jnp.where(lb < qi - 1, cvl, jnp.where(lb > qi + 1, cvr, 0.0)), 0.0).astype(BF16)
        lhs = jnp.concatenate([jnp.concatenate([q1, ext], axis=1),
                               jnp.concatenate([q2, ext], axis=1)], axis=0)
        s_ref[sub] = _dot_nt(lhs, kext_ref[...])

    for sub in range(nsub):
        qi = qt * nsub + sub
        for dd in (-1, 0, 1):
            cb = qi + dd
            cbc = jnp.clip(cb, 0, nblk - 1)
            cols = pl.ds(pl.multiple_of(cbc * qb, qb), qb)
            tile = band_ref[0, dd + 1] * jnp.where(cb == cbc, 1.0, 0.0)
            s_ref[sub, 0:qb, cols] += tile
            s_ref[sub, qb:2 * qb, cols] += tile

    for sub in range(nsub):
        sc = s_ref[sub]
        mx = jnp.max(sc, axis=-1, keepdims=True)
        pe = jnp.exp2(sc - mx).astype(BF16)
        ov = _dot(pe, vext_ref[...])
        o = ov[:qb, :LANES] / ov[:qb, LANES:] - lam * (ov[qb:, :LANES] / ov[qb:, LANES:])
        o = o * lax.rsqrt(jnp.mean(o * o, axis=-1, keepdims=True) + SUBLN_EPS)
        o = o * sg_ref[...] * (1.0 - lambda_init)
        o_ref[0, sub * qb:(sub + 1) * qb, :] = o.astype(o_ref.dtype)


def _diff_attention(qkv, lam_vecs, band, cvec, subln_g, lambda_init, nsub):
    b, s, dw3 = qkv.shape
    dw = dw3 // 3
    nh = dw // LANES
    nblk = s // LANES
    nsub = min(nsub, nblk)
    tq = nsub * LANES
    vec = pl.BlockSpec((1, HEAD), lambda bi, h, t: (0, 0))
    return pl.pallas_call(
        functools.partial(_attn_kernel, nsub=nsub, nblk=nblk, lambda_init=lambda_init),
        grid=(b, nh, s // tq),
        in_specs=[vec] * 4 + [pl.BlockSpec((1, tq, LANES), lambda bi, h, t: (bi, t, h)),
                              pl.BlockSpec((1, s, LANES), lambda bi, h, t: (bi, 0, nh + h)),
                              pl.BlockSpec((1, s, LANES), lambda bi, h, t: (bi, 0, 2 * nh + h)),
                              pl.BlockSpec((1, 3, LANES, LANES), lambda bi, h, t: (h, 0, 0, 0)),
                              pl.BlockSpec((1, 2, LANES), lambda bi, h, t: (h, 0, 0)),
                              pl.BlockSpec((1, LANES), lambda bi, h, t: (0, 0))],
        out_specs=pl.BlockSpec((1, tq, LANES), lambda bi, h, t: (bi, t, h)),
        out_shape=jax.ShapeDtypeStruct((b, s, dw), BF16),
        scratch_shapes=[pltpu.VMEM((s, 2 * LANES), BF16), pltpu.VMEM((s, 2 * LANES), BF16),
                        pltpu.VMEM((nsub, 2 * LANES, s), F32)],
        compiler_params=_cparams(("parallel", "parallel", "arbitrary")),
        name="diff_attention",
    )(*lam_vecs, qkv, qkv, qkv, band, cvec, subln_g)


def _t5_bucket_table(seq):
    rel = np.arange(-(seq - 1), seq, dtype=np.int64)
    nb = N_BUCKETS // 2
    max_exact = nb // 2
    ret = (rel > 0).astype(np.int64) * nb
    n = np.abs(rel)
    nf = np.maximum(n, 1).astype(np.float32)
    scaled = np.log(nf / np.float32(max_exact)) / np.float32(math.log(MAX_DISTANCE / max_exact)) * np.float32(nb - max_exact)
    large = np.minimum(max_exact + scaled.astype(np.int64), nb - 1)
    return ret + np.where(n < max_exact, n, large)


def _attn_bias_operands(rel_bias, seq):
    bucket = _t5_bucket_table(seq)
    centre = seq - 1
    far = LANES + 1
    assert (bucket[:centre - far + 1] == bucket[0]).all() and (bucket[centre + far:] == bucket[-1]).all()
    i = np.arange(LANES)[:, None]
    j = np.arange(LANES)[None, :]
    tiles = []
    for dd in (-1, 0, 1):
        rel = np.clip(dd * LANES + j - i, -(seq - 1), seq - 1)
        tiles.append(bucket[rel + centre])
    idx = jnp.asarray(np.stack(tiles), jnp.int32)
    table = rel_bias.astype(F32) * LOG2E
    band = jnp.transpose(table[idx], (3, 0, 1, 2))
    consts = jnp.stack([table[int(bucket[0])], table[int(bucket[-1])]], axis=1)
    hi = consts.astype(BF16).astype(F32)
    lo = consts - hi
    nh = table.shape[1]
    cvec = jnp.concatenate([jnp.broadcast_to(hi[:, :, None], (nh, 2, N_BUCKETS)),
                            jnp.broadcast_to(lo[:, :, None], (nh, 2, N_BUCKETS)),
                            jnp.zeros((nh, 2, LANES - 2 * N_BUCKETS), F32)], axis=-1)
    return band, cvec


def _outproj_kernel(a_ref, b_ref, wa_ref, wb_ref, x_ref, o_ref):
    o_ref[...] = x_ref[...] + _dot(a_ref[...], wa_ref[...]) + _dot(b_ref[...], wb_ref[...])


def _outproj(a, b, wa, wb, x, tm, tn):
    t, d = x.shape
    tm, tn = _tile(t, tm, 8), _tile(d, tn, LANES)
    ka, kb = a.shape[1], b.shape[1]
    return pl.pallas_call(
        _outproj_kernel,
        grid=(t // tm, d // tn),
        in_specs=[pl.BlockSpec((tm, ka), lambda i, j: (i, 0)),
                  pl.BlockSpec((tm, kb), lambda i, j: (i, 0)),
                  pl.BlockSpec((ka, tn), lambda i, j: (0, j)),
                  pl.BlockSpec((kb, tn), lambda i, j: (0, j)),
                  pl.BlockSpec((tm, tn), lambda i, j: (i, j))],
        out_specs=pl.BlockSpec((tm, tn), lambda i, j: (i, j)),
        out_shape=jax.ShapeDtypeStruct((t, d), F32),
        compiler_params=_cparams(("parallel", "parallel")),
        name="outproj_residual",
    )(a, b, wa, wb, x)


def _router_kernel(x_ref, g_ref, wh_ref, wl_ref, h_ref, affc_ref, affr_ref, *, n_exp):
    x = x_ref[...]
    ms = jnp.mean(x * x, axis=-1, keepdims=True)
    h = x * lax.rsqrt(ms + RMS_EPS) * g_ref[...]
    h_ref[...] = h
    hh, hl = _split2(h)
    wh = wh_ref[...]
    logits = _dot(hh, wh) + _dot(hl, wh) + _dot(hh, wl_ref[...])
    lane = lax.broadcasted_iota(jnp.int32, logits.shape, 1)
    logits = jnp.where(lane < n_exp, logits, -1e30)
    e = jnp.exp(logits - jnp.max(logits, axis=-1, keepdims=True))
    aff = e / jnp.sum(e, axis=-1, keepdims=True)
    affc_ref[...] = aff
    affr_ref[0] = aff.T[:affr_ref.shape[1]]


def _router(x, g, wh, wl, bsz, seq, n_exp, tm):
    t, d = x.shape
    tm = min(tm, seq)
    ep = -(-n_exp // 8) * 8
    tps = seq // tm
    return pl.pallas_call(
        functools.partial(_router_kernel, n_exp=n_exp),
        grid=(t // tm,),
        in_specs=[pl.BlockSpec((tm, d), lambda i: (i, 0)),
                  pl.BlockSpec((1, d), lambda i: (0, 0)),
                  pl.BlockSpec((d, LANES), lambda i: (0, 0)),
                  pl.BlockSpec((d, LANES), lambda i: (0, 0))],
        out_specs=[pl.BlockSpec((tm, d), lambda i: (i, 0)),
                   pl.BlockSpec((tm, LANES), lambda i: (i, 0)),
                   pl.BlockSpec((1, ep, tm), lambda i: (i // tps, 0, i % tps))],
        out_shape=[jax.ShapeDtypeStruct((t, d), F32),
                   jax.ShapeDtypeStruct((t, LANES), F32),
                   jax.ShapeDtypeStruct((bsz, ep, seq), F32)],
        compiler_params=_cparams(("parallel",)),
        name="norm_router",
    )(x, g, wh, wl)


def _topk_kernel(affr_ref, affc_ref, idx_ref, gate_ref, colb_ref, rowb_ref, rank_ref, *, cap, tj):
    e = pl.program_id(1)
    seq = affc_ref.shape[0]
    nt = seq // tj
    a_row = affr_ref[0, pl.ds(e, 1), :]
    lane = lax.broadcasted_iota(jnp.int32, (1, LANES), 1)
    a_col = jnp.sum(jnp.where(lane == e, affc_ref[...], 0.0), axis=1, keepdims=True)
    colb_ref[...] = lax.bitcast_convert_type(a_col, jnp.int32)
    rowb_ref[...] = lax.bitcast_convert_type(a_row, jnp.int32)
    rank_ref[...] = jnp.zeros_like(rank_ref)
    jsub = lax.broadcasted_iota(jnp.int32, (tj, 1), 0)
    ilane = lax.broadcasted_iota(jnp.int32, (1, tj), 1)
    fold = lambda m: jnp.sum(m.reshape(tj // 8, 8, tj), axis=0)

    def count(jt, carry):
        off = pl.multiple_of(jt * tj, tj)
        bj = colb_ref[pl.ds(off, tj), :]
        bj_after = bj - 1
        for ic in range(nt):
            cols = slice(ic * tj, (ic + 1) * tj)
            ahead = jnp.where(ic > jt, bj, bj_after) >= rowb_ref[:, cols]
            rank_ref[:, cols] += fold(jnp.where(ahead, 1.0, 0.0))
        bi = rowb_ref[:, pl.ds(off, tj)]
        tie = jnp.where(bj == bi, jnp.where(jsub < ilane, 1.0, 0.0), 0.0)
        rank_ref[:, pl.ds(off, tj)] += fold(tie)
        return carry

    lax.fori_loop(0, nt, count, 0)
    rank = jnp.sum(rank_ref[...], axis=0, keepdims=True)
    iidx = lax.broadcasted_iota(jnp.int32, (1, seq), 1)
    iidx_f = iidx.astype(F32)
    sb = min(cap, LANES)
    for s0 in range(0, cap, sb):
        slot = (s0 + lax.broadcasted_iota(jnp.int32, (sb, 1), 0)).astype(F32)
        hit = rank == slot
        idx_ref[0, 0, s0:s0 + sb, :] = jnp.sum(jnp.where(hit, iidx_f, 0.0), axis=1, keepdims=True).astype(jnp.int32)
        gate_ref[0, 0, s0:s0 + sb, :] = jnp.sum(jnp.where(hit, a_row, 0.0), axis=1, keepdims=True)


def _topk(affr, affc, n_exp, cap, tj):
    bsz, ep, seq = affr.shape
    tj = min(tj, seq)
    out = pl.BlockSpec((1, 1, cap, 1), lambda b, e: (b, e, 0, 0))
    out_em = pl.BlockSpec((1, 1, cap, 1), lambda b, e: (e, b, 0, 0))
    return pl.pallas_call(
        functools.partial(_topk_kernel, cap=cap, tj=tj),
        grid=(bsz, n_exp),
        in_specs=[pl.BlockSpec((1, ep, seq), lambda b, e: (b, 0, 0)),
                  pl.BlockSpec((seq, LANES), lambda b, e: (b, 0))],
        out_specs=[out, out_em],
        out_shape=[jax.ShapeDtypeStruct((bsz, n_exp, cap, 1), jnp.int32),
                   jax.ShapeDtypeStruct((n_exp, bsz, cap, 1), F32)],
        scratch_shapes=[pltpu.VMEM((seq, 1), jnp.int32), pltpu.VMEM((1, seq), jnp.int32),
                        pltpu.VMEM((8, seq), F32)],
        compiler_params=_cparams(("parallel", "arbitrary")),
        name="expert_choice_rank",
    )(affr, affc)


def _row_copy(src_hbm, dst, src_row, dst_row, sem):
    return pltpu.make_async_copy(src_hbm.at[pl.ds(src_row, 1), :], dst.at[pl.ds(dst_row, 1), :], sem)


def _gather_kernel(idx_ref, h_hbm, o_ref, buf, sem, *, cap, seq, n_exp, bsz):
    b = pl.program_id(0) % bsz
    g = b * n_exp + pl.program_id(0) // bsz
    base = b * seq

    def issue(j, c):
        _row_copy(h_hbm, buf, base + idx_ref[g * cap + j], j, sem).start()
        return c

    lax.fori_loop(0, cap, issue, 0)

    def wait(j, c):
        _row_copy(h_hbm, buf, 0, j, sem).wait()
        return c

    lax.fori_loop(0, cap, wait, 0)
    o_ref[...] = buf[...].astype(o_ref.dtype)


def _gather(idx_flat, h, bsz, cap, seq, n_exp):
    groups = bsz * n_exp
    d = h.shape[1]
    return pl.pallas_call(
        functools.partial(_gather_kernel, cap=cap, seq=seq, n_exp=n_exp, bsz=bsz),
        grid_spec=pltpu.PrefetchScalarGridSpec(
            num_scalar_prefetch=1,
            grid=(groups,),
            in_specs=[pl.BlockSpec(memory_space=pl.ANY)],
            out_specs=pl.BlockSpec((cap, d), lambda g, idx: (g, 0)),
            scratch_shapes=[pltpu.VMEM((cap, d), F32), pltpu.SemaphoreType.DMA(())]),
        out_shape=jax.ShapeDtypeStruct((groups * cap, d), BF16),
        compiler_params=_cparams(("arbitrary",)),
        name="expert_gather",
    )(idx_flat, h)


def _ffn_up_kernel(x_ref, wg_ref, wu_ref, o_ref):
    x = x_ref[...]
    hg = _dot(x, wg_ref[0].astype(BF16))
    hu = _dot(x, wu_ref[0].astype(BF16))
    o_ref[...] = (hg * jax.nn.sigmoid(hg) * hu).astype(o_ref.dtype)


def _ffn_down_kernel(h_ref, wd_ref, gate_ref, o_ref):
    o_ref[...] = _dot(h_ref[...], wd_ref[0].astype(BF16)) * gate_ref[0]


def _expert_ffn(xe, wg, wu, wd, gate, tm, tn_up, tn_down):
    n_exp, d, ff = wg.shape
    rows = xe.shape[0] // n_exp
    tm = _tile(rows, tm, 8)
    mt = rows // tm
    tn_up, tn_down = _tile(ff, tn_up, LANES), _tile(d, tn_down, LANES)
    hid = pl.pallas_call(
        _ffn_up_kernel,
        grid=(n_exp, mt, ff // tn_up),
        in_specs=[pl.BlockSpec((tm, d), lambda e, m, j: (e * mt + m, 0)),
                  pl.BlockSpec((1, d, tn_up), lambda e, m, j: (e, 0, j)),
                  pl.BlockSpec((1, d, tn_up), lambda e, m, j: (e, 0, j))],
        out_specs=pl.BlockSpec((tm, tn_up), lambda e, m, j: (e * mt + m, j)),
        out_shape=jax.ShapeDtypeStruct((xe.shape[0], ff), BF16),
        compiler_params=_cparams(("parallel", "parallel", "arbitrary")),
        name="expert_swiglu_up",
    )(xe, wg, wu)
    return pl.pallas_call(
        _ffn_down_kernel,
        grid=(n_exp, d // tn_down),
        in_specs=[pl.BlockSpec((rows, ff), lambda e, j: (e, 0)),
                  pl.BlockSpec((1, ff, tn_down), lambda e, j: (e, 0, j)),
                  pl.BlockSpec((1, rows, 1), lambda e, j: (e, 0, 0))],
        out_specs=pl.BlockSpec((rows, tn_down), lambda e, j: (e, j)),
        out_shape=jax.ShapeDtypeStruct((xe.shape[0], d), F32),
        compiler_params=_cparams(("parallel", "arbitrary")),
        name="expert_swiglu_down",
    )(hid, wd, gate)


def _combine_kernel(idx_ref, ye_ref, acc_in_hbm, acc_hbm, buf, sem_in, sem_out, *, cap, seq, n_exp):
    del acc_in_hbm
    g = pl.program_id(1) * n_exp + pl.program_id(0)
    base = pl.program_id(1) * seq

    def fetch(j, c):
        _row_copy(acc_hbm, buf, base + idx_ref[g * cap + j], j, sem_in).start()
        return c

    lax.fori_loop(0, cap, fetch, 0)

    def fetch_wait(j, c):
        _row_copy(acc_hbm, buf, 0, j, sem_in).wait()
        return c

    lax.fori_loop(0, cap, fetch_wait, 0)
    buf[...] += ye_ref[...]

    def put(j, c):
        tok = base + idx_ref[g * cap + j]
        pltpu.make_async_copy(buf.at[pl.ds(j, 1), :], acc_hbm.at[pl.ds(tok, 1), :], sem_out).start()
        return c

    lax.fori_loop(0, cap, put, 0)

    def put_wait(j, c):
        pltpu.make_async_copy(buf.at[pl.ds(j, 1), :], acc_hbm.at[pl.ds(0, 1), :], sem_out).wait()
        return c

    lax.fori_loop(0, cap, put_wait, 0)


def _combine(idx_flat, ye, acc, bsz, cap, seq, n_exp):
    d = acc.shape[1]
    return pl.pallas_call(
        functools.partial(_combine_kernel, cap=cap, seq=seq, n_exp=n_exp),
        grid_spec=pltpu.PrefetchScalarGridSpec(
            num_scalar_prefetch=1,
            grid=(n_exp, bsz),
            in_specs=[pl.BlockSpec((cap, d), lambda e, b, idx: (e * bsz + b, 0)),
                      pl.BlockSpec(memory_space=pl.ANY)],
            out_specs=pl.BlockSpec(memory_space=pl.ANY),
            scratch_shapes=[pltpu.VMEM((cap, d), F32), pltpu.SemaphoreType.DMA(()), pltpu.SemaphoreType.DMA(())]),
        out_shape=jax.ShapeDtypeStruct(acc.shape, acc.dtype),
        input_output_aliases={2: 0},
        compiler_params=_cparams(("arbitrary", "arbitrary")),
        name="expert_scatter_add",
    )(idx_flat, ye, acc)


def _rmsnorm_kernel(x_ref, g_ref, o_ref):
    x = x_ref[...]
    ms = jnp.mean(x * x, axis=-1, keepdims=True)
    o_ref[...] = x * lax.rsqrt(ms + RMS_EPS) * g_ref[...]


def _rmsnorm(x, g, tm):
    t, d = x.shape
    tm = min(tm, t)
    return pl.pallas_call(
        _rmsnorm_kernel,
        grid=(t // tm,),
        in_specs=[pl.BlockSpec((tm, d), lambda i: (i, 0)), pl.BlockSpec((1, d), lambda i: (0, 0))],
        out_specs=pl.BlockSpec((tm, d), lambda i: (i, 0)),
        out_shape=jax.ShapeDtypeStruct((t, d), F32),
        compiler_params=_cparams(("parallel",)),
        name="final_rmsnorm",
    )(x, g)


def _pad_to(n, m):
    return -(-n // m) * m


def _pad_axis(a, axis, new):
    pad = [(0, 0)] * a.ndim
    pad[axis] = (0, new - a.shape[axis])
    return jnp.pad(a, pad)


def kernel(x, norm1_g, w_in, mu_prev, mu_next, w0_f, w0_b, wd_up_f, wd_up_b, a0, a_up, g_up_f, g_up_b, k_k, k_a, r_k, lnx_w, lnx_b, lam_q1, lam_k1, lam_q2, lam_k2, subln_g, w_out, norm2_g, w_router, ex_gate, ex_up, ex_down, rel_bias, final_g):
    bsz, seq, d = x.shape
    depth = norm1_g.shape[0]
    width = w0_f.shape[-1]
    dl, al, gl = wd_up_f.shape[1], a_up.shape[1], g_up_f.shape[1]
    dlp, alp, glp = (_pad_to(n, LANES) for n in (dl, al, gl))
    dw = d - width
    n_exp = w_router.shape[-1]
    cap = CAPACITY_FACTOR * seq // n_exp
    t = bsz * seq
    assert seq // LANES <= N_BUCKETS and width % PAIR == 0 and dw % LANES == 0 and seq % CHUNK == 0

    lane = np.arange(LANES)
    bd = jnp.asarray((lane[:, None] // HEAD) == (lane[None, :] // HEAD), BF16)
    row = lambda v: v.reshape(1, -1).astype(F32)
    xt = x.reshape(t, d)
    band, cvec = _attn_bias_operands(rel_bias, seq)

    for l in range(depth):
        o = 3 * width
        segs = [(o, dl, dlp), (o + dl, al, alp), (o + dl + al, gl, glp)]
        w_r = jnp.concatenate([w_in[l][:, :o]] + [_pad_axis(w_in[l][:, s:s + n], 1, npad) for s, n, npad in segs],
                              axis=1).astype(BF16)
        w_d = w_in[l][:, o + dl + al + gl:].astype(BF16)
        padmu = lambda mu: jnp.concatenate([mu[:o]] + [_pad_axis(mu[s:s + n], 0, npad) for s, n, npad in segs]
                                           ).reshape(1, -1)
        g1 = row(norm1_g[l])
        qscale = jnp.concatenate([jnp.full((1, dw), LOG2E / math.sqrt(HEAD), F32), jnp.ones((1, 2 * dw), F32)], axis=1)
        p_rwkv = _norm_matmul(xt, g1, w_r, jnp.ones((1, w_r.shape[1]), F32), F32, 512, 512)
        p_diff = _norm_matmul(xt, g1, w_d, qscale, BF16, 512, 1024)

        r, k2, v, kk, a, lw, gf, gb, bonus = _rwkv_prep(
            p_rwkv, seq, width, dlp, alp, padmu(mu_prev[l]), padmu(mu_next[l]), row(w0_f[l]), row(w0_b[l]),
            _pad_axis(wd_up_f[l], 0, dlp).astype(BF16), _pad_axis(wd_up_b[l], 0, dlp).astype(BF16),
            row(a0[l]), _pad_axis(a_up[l], 0, alp).astype(BF16),
            _pad_axis(g_up_f[l], 0, glp).astype(BF16), _pad_axis(g_up_b[l], 0, glp).astype(BF16),
            row(k_k[l]), row(k_a[l]), row(r_k[l]), bd, 128)
        sh = (bsz, seq, width)
        y = _wkv(r.reshape(sh), k2.reshape(sh), v.reshape(sh), kk.reshape(sh), a.reshape(sh),
                 lw.reshape(2, bsz, seq, width), 4, 16)
        o_rwkv = _rwkv_post(y.reshape(2, t, width), gf, gb, bonus, row(lnx_w[l]), row(lnx_b[l]), bd, 256)

        lambda_init = 0.8 - 0.6 * math.exp(-0.3 * l)
        lam_vecs = [row(lam_q1[l]), row(lam_k1[l]), row(lam_q2[l]), row(lam_k2[l])]
        o_diff = _diff_attention(p_diff.reshape(bsz, seq, 3 * dw), lam_vecs, band, cvec,
                                 row(subln_g[l]), lambda_init, 4)

        wo = w_out[l].astype(BF16)
        xt = _outproj(o_rwkv, o_diff.reshape(t, dw), wo[:width], wo[width:], xt, 1024, 512)

        wr = _pad_axis(w_router[l].astype(F32), 1, LANES)
        wr_hi = wr.astype(BF16)
        wr_lo = (wr - wr_hi.astype(F32)).astype(BF16)
        h2, affc, affr = _router(xt, row(norm2_g[l]), wr_hi, wr_lo, bsz, seq, n_exp, 512)
        idx, gate = _topk(affr, affc, n_exp, cap, 128)
        idx_flat = idx.reshape(-1)
        xe = _gather(idx_flat, h2, bsz, cap, seq, n_exp)
        ye = _expert_ffn(xe, ex_gate[l], ex_up[l], ex_down[l], gate.reshape(n_exp, bsz * cap, 1), 1024, 256, 512)
        xt = _combine(idx_flat, ye, xt, bsz, cap, seq, n_exp)

    return _rmsnorm(xt, row(final_g), 512).reshape(bsz, seq, d)
```
